```python
import math
import jax, jax.numpy as jnp
from jax import lax
import numpy as np

D_MODEL = 1024
BATCH = 8
SEQ = 8192
DEPTH = 1

MEM_LEN = 256
CONV_CH = D_MODEL // 4
CONV_K = 3
DIFF_HEADS = 4
DIFF_QK_DIM = D_MODEL // 16
DIFF_V_DIM = 2 * DIFF_QK_DIM
MEM_HEADS = 4
MEM_HEAD_DIM = D_MODEL // 16
DIFF_QK_TOTAL = DIFF_HEADS * 2 * DIFF_QK_DIM
DIFF_V_TOTAL = DIFF_HEADS * DIFF_V_DIM
MEM_TOTAL = MEM_HEADS * MEM_HEAD_DIM
IN_WIDTH = 3 * CONV_CH + 2 * DIFF_QK_TOTAL + DIFF_V_TOTAL + MEM_TOTAL
MIX_WIDTH = CONV_CH + DIFF_V_TOTAL + MEM_TOTAL
IN_SPLITS = (CONV_CH, 2 * CONV_CH, 3 * CONV_CH,
             3 * CONV_CH + DIFF_QK_TOTAL,
             3 * CONV_CH + 2 * DIFF_QK_TOTAL,
             3 * CONV_CH + 2 * DIFF_QK_TOTAL + DIFF_V_TOTAL)
N_BUCKETS = 32
MAX_DISTANCE = 128
Q_BLOCK = 128
N_EXPERTS = 32
TOP_K = 4
D_EXPERT = D_MODEL
SWIGLU_ALPHA = 1.702
SWIGLU_LIMIT = 7.0
EXPERT_BLOCK = 128
EPS = 1e-5
NEG = -1e30

kernel_name = "hymba_conv_diffattn_mem_moe_block"


def rmsnorm(x, g):
    xf = x.astype(jnp.float32)
    y = xf * lax.rsqrt(jnp.mean(xf * xf, axis=-1, keepdims=True) + EPS)
    return (y * g.astype(jnp.float32)).astype(x.dtype)


def causal_short_conv(u, w):
    rhs = w[:, None, :].astype(u.dtype)
    return lax.conv_general_dilated(
        u, rhs, window_strides=(1,), padding=[(CONV_K - 1, 0)],
        dimension_numbers=("NWC", "WIO", "NWC"), feature_group_count=u.shape[-1])


def t5_bucket(n):
    max_exact = N_BUCKETS // 2
    nf = jnp.maximum(n, 1).astype(jnp.float32)
    large = max_exact + (jnp.log(nf / max_exact) / math.log(MAX_DISTANCE / max_exact)
                         * (N_BUCKETS - max_exact)).astype(jnp.int32)
    large = jnp.minimum(large, N_BUCKETS - 1)
    return jnp.where(n < max_exact, n, large)


def diff_attention(q, k, v, lam, rel_bias):
    S = q.shape[1]
    scale = DIFF_QK_DIM ** -0.5
    outs = []
    for i in range(S // Q_BLOCK):
        q0 = i * Q_BLOCK
        kv_len = q0 + Q_BLOCK
        qb = q[:, q0:kv_len]
        kb = k[:, :kv_len]
        vb = v[:, :kv_len]
        s = jnp.einsum("bqhcd,bkhcd->bhcqk", qb, kb).astype(jnp.float32) * scale
        q_pos = q0 + jnp.arange(Q_BLOCK, dtype=jnp.int32)
        k_pos = jnp.arange(kv_len, dtype=jnp.int32)
        dist = q_pos[:, None] - k_pos[None, :]
        bias = jnp.transpose(rel_bias[t5_bucket(jnp.maximum(dist, 0))], (2, 0, 1)).astype(jnp.float32)
        s = jnp.where((dist >= 0)[None, None, None], s + bias[None, :, None], NEG)
        p = jax.nn.softmax(s, axis=-1)
        p_diff = p[:, :, 0] - lam * p[:, :, 1]
        outs.append(jnp.einsum("bhqk,bkhd->bqhd", p_diff.astype(vb.dtype), vb))
    return jnp.concatenate(outs, axis=1)


def memory_attention(qm, mem_h, w_mem_kv):
    B, M, _ = mem_h.shape
    kv = mem_h @ w_mem_kv
    km, vm = jnp.split(kv, 2, axis=-1)
    km = km.reshape(B, M, MEM_HEADS, MEM_HEAD_DIM)
    vm = vm.reshape(B, M, MEM_HEADS, MEM_HEAD_DIM)
    s = jnp.einsum("bshd,bmhd->bhsm", qm, km).astype(jnp.float32) * (MEM_HEAD_DIM ** -0.5)
    p = jax.nn.softmax(s, axis=-1)
    return jnp.einsum("bhsm,bmhd->bshd", p.astype(vm.dtype), vm)


def moe_ffn(h, router_w, router_b, w_gate_up, b_gate_up, w_down, b_down):
    T, D = h.shape
    logits = (h @ router_w).astype(jnp.float32) + router_b.astype(jnp.float32)
    top_logit, top_idx = lax.top_k(logits, TOP_K)
    gates = jax.nn.softmax(top_logit, axis=-1)
    n_slots = T * TOP_K
    flat_e = top_idx.reshape(-1).astype(jnp.int32)
    flat_tok = jnp.repeat(jnp.arange(T, dtype=jnp.int32), TOP_K)
    flat_g = gates.reshape(-1)
    order = jnp.argsort(flat_e)
    sorted_e = flat_e[order]
    counts = jnp.bincount(flat_e, length=N_EXPERTS).astype(jnp.int32)
    padded = (counts + EXPERT_BLOCK - 1) // EXPERT_BLOCK * EXPERT_BLOCK
    start = jnp.cumsum(counts) - counts
    pad_end = jnp.cumsum(padded)
    pad_start = pad_end - padded
    dest = pad_start[sorted_e] + jnp.arange(n_slots, dtype=jnp.int32) - start[sorted_e]
    n_blocks = -(-n_slots // EXPERT_BLOCK) + N_EXPERTS
    n_pad = n_blocks * EXPERT_BLOCK
    tok_buf = jnp.full((n_pad,), T, jnp.int32).at[dest].set(flat_tok[order])
    gate_buf = jnp.zeros((n_pad,), jnp.float32).at[dest].set(flat_g[order])
    block_e = jnp.searchsorted(pad_end, jnp.arange(n_blocks, dtype=jnp.int32) * EXPERT_BLOCK, side="right")
    block_e = jnp.minimum(block_e, N_EXPERTS - 1).astype(jnp.int32)
    h_pad = jnp.concatenate([h, jnp.zeros((1, D), h.dtype)], axis=0)

    def expert_block(args):
        tok, g, e = args
        xb = h_pad[tok]
        gu = xb @ w_gate_up[e] + b_gate_up[e]
        gate, up = jnp.split(gu, 2, axis=-1)
        gate = jnp.minimum(gate, SWIGLU_LIMIT)
        up = jnp.clip(up, -SWIGLU_LIMIT, SWIGLU_LIMIT)
        act = (up + 1.0) * (gate * jax.nn.sigmoid(SWIGLU_ALPHA * gate))
        y = act @ w_down[e] + b_down[e]
        return y * g[:, None].astype(y.dtype)

    yb = lax.map(expert_block, (tok_buf.reshape(n_blocks, EXPERT_BLOCK),
                                gate_buf.reshape(n_blocks, EXPERT_BLOCK), block_e))
    out = jax.ops.segment_sum(yb.reshape(n_pad, D), tok_buf, num_segments=T + 1)
    return out[:T]


def setup_inputs(seed: int = 0) -> dict:
    key = jax.random.key(seed)
    ks = jax.random.split(key, 20)
    f32 = jnp.float32
    L, D = DEPTH, D_MODEL
    nrm = lambda k, shape, s: jax.random.normal(k, shape, f32) * s
    gain = lambda k, shape: 1.0 + 0.01 * jax.random.normal(k, shape, f32)
    return {
        "x": jax.random.normal(ks[0], (BATCH, SEQ, D), f32),
        "mem": jax.random.normal(ks[1], (BATCH, MEM_LEN, D), f32),
        "rel_bias": nrm(ks[2], (N_BUCKETS, DIFF_HEADS), 0.5),
        "attn_norm_g": gain(ks[3], (L, D)),
        "w_mix_in": nrm(ks[4], (L, D, IN_WIDTH), D ** -0.5),
        "conv_w": nrm(ks[5], (L, CONV_K, CONV_CH), CONV_K ** -0.5),
        "lambda_qk": nrm(ks[6], (L, 4, DIFF_QK_DIM), 0.1),
        "diff_subln_g": gain(ks[7], (L, DIFF_V_DIM)),
        "mem_norm_g": gain(ks[8], (L, D)),
        "w_mem_kv": nrm(ks[9], (L, D, 2 * MEM_TOTAL), D ** -0.5),
        "w_mix_out": nrm(ks[10], (L, MIX_WIDTH, D), MIX_WIDTH ** -0.5),
        "ffn_norm_g": gain(ks[11], (L, D)),
        "router_w": nrm(ks[12], (L, D, N_EXPERTS), D ** -0.5),
        "router_b": nrm(ks[13], (L, N_EXPERTS), 0.01),
        "w_gate_up": nrm(ks[14], (L, N_EXPERTS, D, 2 * D_EXPERT), D ** -0.5),
        "b_gate_up": nrm(ks[15], (L, N_EXPERTS, 2 * D_EXPERT), 0.01),
        "w_down": nrm(ks[16], (L, N_EXPERTS, D_EXPERT, D), D_EXPERT ** -0.5),
        "b_down": nrm(ks[17], (L, N_EXPERTS, D), 0.01),
        "final_norm_g": gain(ks[18], (D,)),
    }


def reference(x, mem, rel_bias, attn_norm_g, w_mix_in, conv_w, lambda_qk, diff_subln_g,
              mem_norm_g, w_mem_kv, w_mix_out, ffn_norm_g, router_w, router_b,
              w_gate_up, b_gate_up, w_down, b_down, final_norm_g):
    B, S, D = x.shape
    for l in range(DEPTH):
        lam_init = 0.8 - 0.6 * math.exp(-0.3 * l)
        h = rmsnorm(x, attn_norm_g[l])
        proj = h @ w_mix_in[l]
        u, g_b, g_c, q, k, v, qm = jnp.split(proj, IN_SPLITS, axis=-1)
        y_conv = g_b * causal_short_conv(g_c * u, conv_w[l])
        lq = lambda_qk[l].astype(jnp.float32)
        lam = jnp.exp(jnp.sum(lq[0] * lq[1])) - jnp.exp(jnp.sum(lq[2] * lq[3])) + lam_init
        q = q.reshape(B, S, DIFF_HEADS, 2, DIFF_QK_DIM)
        k = k.reshape(B, S, DIFF_HEADS, 2, DIFF_QK_DIM)
        v = v.reshape(B, S, DIFF_HEADS, DIFF_V_DIM)
        o = diff_attention(q, k, v, lam, rel_bias)
        o = rmsnorm(o, diff_subln_g[l]) * (1.0 - lam_init)
        y_diff = o.reshape(B, S, DIFF_V_TOTAL).astype(x.dtype)
        mem_h = rmsnorm(mem, mem_norm_g[l])
        y_mem = memory_attention(qm.reshape(B, S, MEM_HEADS, MEM_HEAD_DIM), mem_h, w_mem_kv[l])
        y_mem = y_mem.reshape(B, S, MEM_TOTAL)
        mix = jnp.concatenate([y_conv, y_diff, y_mem], axis=-1)
        x = x + mix @ w_mix_out[l]
        h = rmsnorm(x, ffn_norm_g[l]).reshape(B * S, D)
        x = x + moe_ffn(h, router_w[l], router_b[l], w_gate_up[l], b_gate_up[l],
                        w_down[l], b_down[l]).reshape(B, S, D)
    return rmsnorm(x, final_norm_g)
```

```python
import functools

import numpy as np
import jax
import jax.numpy as jnp
from jax import lax
from jax.experimental import pallas as pl
from jax.experimental.pallas import tpu as pltpu

F32 = jnp.float32
BF16 = jnp.bfloat16
I32 = jnp.int32

EPS = 1e-5
NEG = -1e30
LANES = 128
SUBLANES = 8
VMEM_LIMIT = 56 * 1024 * 1024

CONV_K = 3
DIFF_HEADS = 4
DIFF_QK_DIM = 64
DIFF_V_DIM = 128
MEM_HEADS = 4
MEM_HEAD_DIM = 64
N_BUCKETS = 32
MAX_DISTANCE = 128
N_EXPERTS = 32
TOP_K = 4
SWIGLU_ALPHA = 1.702
SWIGLU_LIMIT = 7.0
LAM_INIT = 0.8 - 0.6 * 1.0

SEQ_TILE = 512
ATTN_TILE = 256
ROUTE_TILE = 512
MOE_BLOCK = 256
ROW_TILE = 256


def _rms(x, g):
    return x * lax.rsqrt(jnp.mean(x * x, axis=-1, keepdims=True) + EPS) * g


def _cparams(sem):
    return pltpu.CompilerParams(dimension_semantics=sem, vmem_limit_bytes=VMEM_LIMIT)


def _mem_kv_kernel(mem_ref, g_ref, w_ref, km_ref, vm_ref):
    y = _rms(mem_ref[0], g_ref[...])
    kv = jnp.dot(y.astype(BF16), w_ref[...], preferred_element_type=F32)
    half = kv.shape[1] // 2
    km_ref[0] = kv[:, :half].astype(BF16)
    vm_ref[0] = kv[:, half:].astype(BF16)


def _mem_kv(mem, g, w):
    B, M, D = mem.shape
    W2 = w.shape[1]
    return pl.pallas_call(
        _mem_kv_kernel,
        grid=(B,),
        in_specs=[pl.BlockSpec((1, M, D), lambda b: (b, 0, 0)),
                  pl.BlockSpec((1, D), lambda b: (0, 0)),
                  pl.BlockSpec((D, W2), lambda b: (0, 0))],
        out_specs=[pl.BlockSpec((1, M, W2 // 2), lambda b: (b, 0, 0)),
                   pl.BlockSpec((1, M, W2 // 2), lambda b: (b, 0, 0))],
        out_shape=[jax.ShapeDtypeStruct((B, M, W2 // 2), BF16)] * 2,
        compiler_params=_cparams(("arbitrary",)),
        name="mem_kv",
    )(mem, g, w)


def _mix_in_kernel(x_ref, g_ref, w_ref, cw_ref, km_ref, vm_ref,
                   yc_ref, ym_ref, q_ref, k_ref, v_ref, cbuf, *, ts, cc, qk, vw, mw):
    si = pl.program_id(1)
    h = _rms(x_ref[0], g_ref[...])
    proj = jnp.dot(h.astype(BF16), w_ref[...], preferred_element_type=F32)
    u = proj[:, 0:cc]
    gb = proj[:, cc:2 * cc]
    gc = proj[:, 2 * cc:3 * cc]
    cu = gc * u

    @pl.when(si == 0)
    def _():
        cbuf[0:SUBLANES, :] = jnp.zeros((SUBLANES, cc), F32)

    cbuf[SUBLANES:SUBLANES + ts, :] = cu
    w = cw_ref[...]
    y = (w[2:3, :] * cu + w[1:2, :] * cbuf[SUBLANES - 1:SUBLANES - 1 + ts, :]
         + w[0:1, :] * cbuf[SUBLANES - 2:SUBLANES - 2 + ts, :])
    yc_ref[0] = (gb * y).astype(BF16)
    cbuf[0:SUBLANES, :] = cbuf[ts:ts + SUBLANES, :]

    o = 3 * cc
    scale = DIFF_QK_DIM ** -0.5
    q_ref[0] = (proj[:, o:o + qk] * scale).astype(BF16)
    k_ref[0] = proj[:, o + qk:o + 2 * qk].astype(BF16)
    v_ref[0] = proj[:, o + 2 * qk:o + 2 * qk + vw].astype(BF16)
    o = o + 2 * qk + vw
    qm = (proj[:, o:o + mw] * (MEM_HEAD_DIM ** -0.5)).astype(BF16)

    km = km_ref[0]
    vm = vm_ref[0]
    lane = lax.broadcasted_iota(I32, (1, mw), 1)
    acc = jnp.zeros((ts, mw), F32)
    for hh in range(MEM_HEADS):
        msk = (lane >= hh * MEM_HEAD_DIM) & (lane < (hh + 1) * MEM_HEAD_DIM)
        qh = jnp.where(msk, qm, jnp.zeros_like(qm))
        s = lax.dot_general(qh, km, (((1,), (1,)), ((), ())), preferred_element_type=F32)
        p = jnp.exp(s - jnp.max(s, axis=-1, keepdims=True))
        l = jnp.sum(p, axis=-1, keepdims=True)
        oh = jnp.dot(p.astype(BF16), vm, preferred_element_type=F32)
        acc = acc + jnp.where(msk, oh / l, 0.0)
    ym_ref[0] = acc.astype(BF16)


def _mix_in(x, g, w_bf, conv_w, km, vm):
    B, S, D = x.shape
    ts = min(SEQ_TILE, S)
    cc = conv_w.shape[1]
    qk = DIFF_HEADS * 2 * DIFF_QK_DIM
    vw = DIFF_HEADS * DIFF_V_DIM
    mw = MEM_HEADS * MEM_HEAD_DIM
    M = km.shape[1]
    kern = functools.partial(_mix_in_kernel, ts=ts, cc=cc, qk=qk, vw=vw, mw=mw)
    tile = lambda n: pl.BlockSpec((1, ts, n), lambda b, s: (b, s, 0))
    const = lambda shape: pl.BlockSpec(shape, lambda b, s: (0,) * len(shape))
    return pl.pallas_call(
        kern,
        grid=(B, S // ts),
        in_specs=[tile(D), const((1, D)), const(w_bf.shape), const(conv_w.shape),
                  pl.BlockSpec((1, M, mw), lambda b, s: (b, 0, 0)),
                  pl.BlockSpec((1, M, mw), lambda b, s: (b, 0, 0))],
        out_specs=[tile(cc), tile(mw), tile(qk), tile(qk), tile(vw)],
        out_shape=[jax.ShapeDtypeStruct((B, S, n), BF16) for n in (cc, mw, qk, qk, vw)],
        scratch_shapes=[pltpu.VMEM((ts + SUBLANES, cc), F32)],
        compiler_params=_cparams(("arbitrary", "arbitrary")),
        name="mix_in",
    )(x, g, w_bf, conv_w, km, vm)


def _attn_kernel(lam_ref, q_ref, k_ref, v_ref, bias_ref, g_ref, o_ref,
                 m1, l1, a1, m2, l2, a2, *, tq):
    i = pl.program_id(2)
    q = q_ref[0]
    lane = lax.broadcasted_iota(I32, (1, LANES), 1)
    zero = jnp.zeros_like(q)
    qa = jnp.where(lane < DIFF_QK_DIM, q, zero)
    qb = jnp.where(lane >= DIFF_QK_DIM, q, zero)
    for m, l, a in ((m1, l1, a1), (m2, l2, a2)):
        m[...] = jnp.full(m.shape, NEG, F32)
        l[...] = jnp.zeros(l.shape, F32)
        a[...] = jnp.zeros(a.shape, F32)

    def step(j, bias):
        start = pl.multiple_of(j * tq, tq)
        kb = k_ref[0, pl.ds(start, tq), :]
        vb = v_ref[0, pl.ds(start, tq), :]
        for qx, m, l, a in ((qa, m1, l1, a1), (qb, m2, l2, a2)):
            s = lax.dot_general(qx, kb, (((1,), (1,)), ((), ())), preferred_element_type=F32)
            if bias is not None:
                s = s + bias
            m_prev = m[...]
            m_next = jnp.maximum(m_prev, jnp.max(s, axis=1, keepdims=True))
            p = jnp.exp(s - m_next[:, 0:1])
            alpha = jnp.exp(m_prev - m_next)
            l[...] = alpha * l[...] + jnp.sum(p, axis=1, keepdims=True)
            a[...] = alpha * a[...] + jnp.dot(p.astype(BF16), vb, preferred_element_type=F32)
            m[...] = m_next

    def body(j, c):
        step(j, None)
        return c

    lax.fori_loop(0, jnp.maximum(i - 1, 0), body, 0)

    @pl.when(i >= 1)
    def _():
        step(i - 1, bias_ref[0, 1])

    step(i, bias_ref[0, 0])

    lq = lam_ref[...]
    lam = (jnp.exp(jnp.sum(lq[0:1, :] * lq[1:2, :], axis=1, keepdims=True))
           - jnp.exp(jnp.sum(lq[2:3, :] * lq[3:4, :], axis=1, keepdims=True)) + LAM_INIT)
    o = a1[...] / l1[...] - lam * (a2[...] / l2[...])
    o = _rms(o, g_ref[...]) * (1.0 - LAM_INIT)
    o_ref[0] = o.astype(BF16)


def _t5_bucket(n):
    max_exact = N_BUCKETS // 2
    nf = jnp.maximum(n, 1).astype(F32)
    large = max_exact + (jnp.log(nf / max_exact) / np.log(MAX_DISTANCE / max_exact)
                         * (N_BUCKETS - max_exact)).astype(I32)
    large = jnp.minimum(large, N_BUCKETS - 1)
    return jnp.where(n < max_exact, n, large)


def _bias_tiles(rel_bias, t):
    assert t >= MAX_DISTANCE
    dist = jnp.arange(2 * t, dtype=I32)
    tab = rel_bias.astype(F32)[_t5_bucket(dist)] - rel_bias.astype(F32)[N_BUCKETS - 1][None, :]
    r = np.arange(t)[:, None]
    c = np.arange(t)[None, :]
    d0 = r - c
    diag = jnp.where(jnp.asarray(d0 >= 0)[:, :, None], tab[np.maximum(d0, 0)], NEG)
    prev = tab[d0 + t]
    return jnp.transpose(jnp.stack([diag, prev], axis=0), (3, 0, 1, 2))


def _diff_attn(q, k, v, bias, lam_qk, g):
    B, S, _ = q.shape
    t = min(ATTN_TILE, S)
    kern = functools.partial(_attn_kernel, tq=t)
    return pl.pallas_call(
        kern,
        grid=(B, DIFF_HEADS, S // t),
        in_specs=[pl.BlockSpec(lam_qk.shape, lambda b, h, i: (0, 0)),
                  pl.BlockSpec((1, t, LANES), lambda b, h, i: (b, i, h)),
                  pl.BlockSpec((1, S, LANES), lambda b, h, i: (b, 0, h)),
                  pl.BlockSpec((1, S, LANES), lambda b, h, i: (b, 0, h)),
                  pl.BlockSpec((1, 2, t, t), lambda b, h, i: (h, 0, 0, 0)),
                  pl.BlockSpec((1, DIFF_V_DIM), lambda b, h, i: (0, 0))],
        out_specs=pl.BlockSpec((1, t, LANES), lambda b, h, i: (b, i, h)),
        out_shape=jax.ShapeDtypeStruct((B, S, DIFF_HEADS * DIFF_V_DIM), BF16),
        scratch_shapes=[pltpu.VMEM((t, LANES), F32)] * 6,
        compiler_params=_cparams(("arbitrary", "arbitrary", "arbitrary")),
        name="diff_attn",
    )(lam_qk, q, k, v, bias, g)


def _mix_out_kernel(x_ref, yc_ref, yd_ref, ym_ref, w_ref, g_ref, rw_ref, rb_ref,
                    x1_ref, h_ref, idx_ref, gate_ref, *, cc, dw):
    x1 = (x_ref[...]
          + jnp.dot(yc_ref[...], w_ref[0:cc, :], preferred_element_type=F32)
          + jnp.dot(yd_ref[...], w_ref[cc:cc + dw, :], preferred_element_type=F32)
          + jnp.dot(ym_ref[...], w_ref[cc + dw:, :], preferred_element_type=F32))
    x1_ref[...] = x1
    h = _rms(x1, g_ref[...])
    h_ref[...] = h
    logits = jnp.dot(h, rw_ref[...], preferred_element_type=F32,
                     precision=lax.Precision.HIGHEST) + rb_ref[...]
    lane = lax.broadcasted_iota(I32, logits.shape, 1).astype(F32)
    vals, idxs = [], []
    cur = logits
    for _ in range(TOP_K):
        mk = jnp.max(cur, axis=1, keepdims=True)
        ik = jnp.min(jnp.where(cur == mk, lane, float(LANES)), axis=1, keepdims=True)
        vals.append(mk)
        idxs.append(ik)
        cur = jnp.where(lane == ik, -jnp.inf, cur)
    es = [jnp.exp(v - vals[0]) for v in vals]
    den = es[0] + es[1] + es[2] + es[3]
    idx_out = jnp.zeros(logits.shape, F32)
    gate_out = jnp.zeros(logits.shape, F32)
    for kk in range(TOP_K):
        idx_out = jnp.where(lane == float(kk), idxs[kk], idx_out)
        gate_out = jnp.where(lane == float(kk), es[kk] / den, gate_out)
    idx_ref[...] = idx_out.astype(I32)
    gate_ref[...] = gate_out


def _mix_out(x2, yc, yd, ym, w_bf, g, rw_pad, rb_pad):
    T, D = x2.shape
    tm = min(SEQ_TILE, T)
    cc, dw, mw = yc.shape[1], yd.shape[1], ym.shape[1]
    kern = functools.partial(_mix_out_kernel, cc=cc, dw=dw)
    tile = lambda n: pl.BlockSpec((tm, n), lambda i: (i, 0))
    const = lambda shape: pl.BlockSpec(shape, lambda i: (0, 0))
    return pl.pallas_call(
        kern,
        grid=(T // tm,),
        in_specs=[tile(D), tile(cc), tile(dw), tile(mw), const(w_bf.shape), const((1, D)),
                  const(rw_pad.shape), const(rb_pad.shape)],
        out_specs=[tile(D), tile(D), tile(LANES), tile(LANES)],
        out_shape=[jax.ShapeDtypeStruct((T, D), F32), jax.ShapeDtypeStruct((T, D), F32),
                   jax.ShapeDtypeStruct((T, LANES), I32), jax.ShapeDtypeStruct((T, LANES), F32)],
        compiler_params=_cparams(("arbitrary",)),
        name="mix_out",
    )(x2, yc, yd, ym, w_bf, g, rw_pad, rb_pad)


def _route_kernel(idx_ref, pos_ref, meta_ref, cnt, pstart, *, tile, blk):
    ph = pl.program_id(0)
    i = pl.program_id(1)
    lane = lax.broadcasted_iota(I32, (tile, LANES), 1)
    idx = idx_ref[...]
    ohs = [(idx[:, kk:kk + 1] == lane) for kk in range(TOP_K)]
    oh = jnp.zeros((tile, LANES), F32)
    for m in ohs:
        oh = oh + jnp.where(m, 1.0, 0.0)
    colsum = jnp.sum(oh, axis=0, keepdims=True)

    @pl.when((ph == 0) & (i == 0))
    def _():
        cnt[...] = jnp.zeros(cnt.shape, F32)

    @pl.when(ph == 0)
    def _():
        cnt[...] = cnt[...] + colsum

    @pl.when((ph == 1) & (i == 0))
    def _():
        total = cnt[...]
        padded = jnp.floor((total + (blk - 1)) * (1.0 / blk)) * blk
        l8 = lax.broadcasted_iota(I32, total.shape, 1)
        end = padded
        sh = 1
        while sh < LANES:
            end = end + jnp.where(l8 >= sh, pltpu.roll(end, sh, axis=1), 0.0)
            sh *= 2
        pstart[...] = end - padded
        row = lax.broadcasted_iota(I32, total.shape, 0)
        meta_ref[...] = jnp.where(row == 0, total, jnp.where(row == 1, end - padded, end))
        cnt[...] = jnp.zeros(cnt.shape, F32)

    @pl.when(ph == 1)
    def _():
        r = lax.broadcasted_iota(I32, (tile, tile), 0)
        c = lax.broadcasted_iota(I32, (tile, tile), 1)
        tri = jnp.where(c < r, 1.0, 0.0).astype(BF16)
        before = jnp.dot(tri, oh.astype(BF16), preferred_element_type=F32)
        base = before + cnt[0:1, :] + pstart[0:1, :]
        out = jnp.zeros((tile, LANES), F32)
        for kk in range(TOP_K):
            pk = jnp.sum(jnp.where(ohs[kk], base, 0.0), axis=1, keepdims=True)
            out = jnp.where(lane == kk, pk, out)
        pos_ref[...] = out.astype(I32)
        cnt[...] = cnt[...] + colsum


def _route(idx, blk):
    T = idx.shape[0]
    tile = min(ROUTE_TILE, T)
    kern = functools.partial(_route_kernel, tile=tile, blk=blk)
    return pl.pallas_call(
        kern,
        grid=(2, T // tile),
        in_specs=[pl.BlockSpec((tile, LANES), lambda p, i: (i, 0))],
        out_specs=[pl.BlockSpec((tile, LANES), lambda p, i: (p * i, 0)),
                   pl.BlockSpec((SUBLANES, LANES), lambda p, i: (0, 0))],
        out_shape=[jax.ShapeDtypeStruct((T, LANES), I32),
                   jax.ShapeDtypeStruct((SUBLANES, LANES), F32)],
        scratch_shapes=[pltpu.VMEM((SUBLANES, LANES), F32), pltpu.VMEM((SUBLANES, LANES), F32)],
        compiler_params=_cparams(("arbitrary", "arbitrary")),
        name="route",
    )(idx)


def _row_copy(src, si, dst, di, sem):
    return pltpu.make_async_copy(src.at[pl.ds(si, 1), :], dst.at[pl.ds(di, 1), :], sem)


def _dispatch_kernel(cnt_ref, pst_ref, dest_ref, h_ref, xs_ref, zbuf, sem, *, tm, blk):
    def issue(r, c):
        for kk in range(TOP_K):
            _row_copy(h_ref, r, xs_ref, dest_ref[r * TOP_K + kk], sem).start()
        return c

    lax.fori_loop(0, tm, issue, 0)

    def drain(r, c):
        for kk in range(TOP_K):
            _row_copy(h_ref, 0, xs_ref, 0, sem).wait()
        return c

    lax.fori_loop(0, tm, drain, 0)

    @pl.when(pl.program_id(0) == pl.num_programs(0) - 1)
    def _():
        zbuf[...] = jnp.zeros(zbuf.shape, F32)

        def per_expert(e, c):
            n = cnt_ref[e]
            base = pst_ref[e]
            n_pad = lax.shift_right_logical(n + (blk - 1), int(np.log2(blk))) * blk

            def fill(r, c2):
                cp = _row_copy(zbuf, 0, xs_ref, base + r, sem)
                cp.start()
                cp.wait()
                return c2

            lax.fori_loop(n, n_pad, fill, 0)
            return c

        lax.fori_loop(0, N_EXPERTS, per_expert, 0)


def _dispatch(counts, pstart, dest, h, n_pad, blk):
    T, D = h.shape
    tm = min(ROW_TILE, T)
    kern = functools.partial(_dispatch_kernel, tm=tm, blk=blk)
    return pl.pallas_call(
        kern,
        grid_spec=pltpu.PrefetchScalarGridSpec(
            num_scalar_prefetch=2,
            grid=(T // tm,),
            in_specs=[pl.BlockSpec((tm * TOP_K,), lambda i, *_: (i,), memory_space=pltpu.SMEM),
                      pl.BlockSpec((tm, D), lambda i, *_: (i, 0))],
            out_specs=pl.BlockSpec(memory_space=pl.ANY),
            scratch_shapes=[pltpu.VMEM((SUBLANES, D), F32), pltpu.SemaphoreType.DMA(())]),
        out_shape=jax.ShapeDtypeStruct((n_pad, D), F32),
        compiler_params=_cparams(("arbitrary",)),
        name="dispatch",
    )(counts, pstart, dest, h)


def _expert_kernel(be_ref, nv_ref, x_ref, wgu_ref, bgu_ref, wd_ref, bd_ref, y_ref, *, de):
    @pl.when(pl.program_id(0) < nv_ref[0])
    def _():
        gu = jnp.dot(x_ref[...].astype(BF16), wgu_ref[0], preferred_element_type=F32) + bgu_ref[0]
        gate = jnp.minimum(gu[:, :de], SWIGLU_LIMIT)
        up = jnp.clip(gu[:, de:], -SWIGLU_LIMIT, SWIGLU_LIMIT)
        act = (up + 1.0) * (gate * jax.nn.sigmoid(SWIGLU_ALPHA * gate))
        y_ref[...] = jnp.dot(act.astype(BF16), wd_ref[0], preferred_element_type=F32) + bd_ref[0]


def _expert_ffn(block_e, nvalid, xs, wgu, bgu, wd, bd, blk):
    n_pad, D = xs.shape
    de = wd.shape[1]
    kern = functools.partial(_expert_kernel, de=de)
    row = lambda j, be, nv: (jnp.minimum(j, nv[0] - 1), 0)
    exp = lambda j, be, nv: (be[jnp.minimum(j, nv[0] - 1)], 0, 0)
    return pl.pallas_call(
        kern,
        grid_spec=pltpu.PrefetchScalarGridSpec(
            num_scalar_prefetch=2,
            grid=(n_pad // blk,),
            in_specs=[pl.BlockSpec((blk, D), row),
                      pl.BlockSpec((1, D, 2 * de), exp),
                      pl.BlockSpec((1, 1, 2 * de), exp),
                      pl.BlockSpec((1, de, D), exp),
                      pl.BlockSpec((1, 1, D), exp)],
            out_specs=pl.BlockSpec((blk, D), row)),
        out_shape=jax.ShapeDtypeStruct((n_pad, D), F32),
        compiler_params=_cparams(("arbitrary",)),
        name="expert_ffn",
    )(block_e, nvalid, xs, wgu, bgu, wd, bd)


def _combine_kernel(dest_ref, x1_ref, gate_ref, g_ref, y_ref, o_ref, ybuf, sem, *, tm):
    def issue(r, c):
        for kk in range(TOP_K):
            _row_copy(y_ref, dest_ref[r * TOP_K + kk], ybuf.at[kk], r, sem).start()
        return c

    lax.fori_loop(0, tm, issue, 0)

    def drain(r, c):
        for kk in range(TOP_K):
            _row_copy(y_ref, 0, ybuf.at[kk], 0, sem).wait()
        return c

    lax.fori_loop(0, tm, drain, 0)

    gates = gate_ref[...]
    acc = x1_ref[...]
    for kk in range(TOP_K):
        acc = acc + gates[:, kk:kk + 1] * ybuf[kk]
    o_ref[...] = _rms(acc, g_ref[...])


def _combine(dest, x1, gates, g, y):
    T, D = x1.shape
    tm = min(ROW_TILE, T)
    kern = functools.partial(_combine_kernel, tm=tm)
    return pl.pallas_call(
        kern,
        grid=(T // tm,),
        in_specs=[pl.BlockSpec((tm * TOP_K,), lambda i: (i,), memory_space=pltpu.SMEM),
                  pl.BlockSpec((tm, D), lambda i: (i, 0)),
                  pl.BlockSpec((tm, LANES), lambda i: (i, 0)),
                  pl.BlockSpec((1, D), lambda i: (0, 0)),
                  pl.BlockSpec(memory_space=pl.ANY)],
        out_specs=pl.BlockSpec((tm, D), lambda i: (i, 0)),
        out_shape=jax.ShapeDtypeStruct((T, D), F32),
        scratch_shapes=[pltpu.VMEM((TOP_K, tm, D), F32), pltpu.SemaphoreType.DMA(())],
        compiler_params=_cparams(("arbitrary",)),
        name="combine",
    )(dest, x1, gates, g, y)


def kernel(x, mem, rel_bias, attn_norm_g, w_mix_in, conv_w, lambda_qk, diff_subln_g, mem_norm_g,
           w_mem_kv, w_mix_out, ffn_norm_g, router_w, router_b, w_gate_up, b_gate_up, w_down,
           b_down, final_norm_g):
    B, S, D = x.shape
    T = B * S
    E = router_w.shape[-1]
    assert w_mix_in.shape[0] == 1 and E == N_EXPERTS

    km, vm = _mem_kv(mem, mem_norm_g[0][None, :], w_mem_kv[0].astype(BF16))
    yc, ym, q, k, v = _mix_in(x, attn_norm_g[0][None, :], w_mix_in[0].astype(BF16), conv_w[0], km, vm)

    t = min(ATTN_TILE, S)
    bias = _bias_tiles(rel_bias, t)
    yd = _diff_attn(q, k, v, bias, lambda_qk[0], diff_subln_g[0][None, :])

    rw_pad = jnp.zeros((D, LANES), F32).at[:, :E].set(router_w[0])
    rb_pad = jnp.full((1, LANES), NEG, F32).at[0, :E].set(router_b[0])
    x1, h, idx, gates = _mix_out(x.reshape(T, D), yc.reshape(T, -1), yd.reshape(T, -1),
                                 ym.reshape(T, -1), w_mix_out[0].astype(BF16),
                                 ffn_norm_g[0][None, :], rw_pad, rb_pad)

    blk = MOE_BLOCK
    n_blocks = (T * TOP_K) // blk + E
    n_pad = n_blocks * blk
    pos, meta = _route(idx, blk)
    dest = pos[:, :TOP_K].reshape(T * TOP_K)
    counts = meta[0, :E].astype(I32)
    pstart = meta[1, :E].astype(I32)
    pend = meta[2, :E].astype(I32)
    block_start = jnp.arange(n_blocks, dtype=I32) * blk
    block_e = jnp.minimum(jnp.sum(pend[None, :] <= block_start[:, None], axis=1), E - 1).astype(I32)
    nvalid = (pend[E - 1:E] // blk).astype(I32)

    xs = _dispatch(counts, pstart, dest, h, n_pad, blk)
    y = _expert_ffn(block_e, nvalid, xs, w_gate_up[0].astype(BF16), b_gate_up[0][:, None, :],
                    w_down[0].astype(BF16), b_down[0][:, None, :], blk)
    out = _combine(dest, x1, gates, final_norm_g[None, :], y)
    return out.reshape(B, S, D)
```

```python
import functools

import numpy as np
import jax
import jax.numpy as jnp
from jax import lax
from jax.experimental import pallas as pl
from jax.experimental.pallas import tpu as pltpu

F32 = jnp.float32
BF16 = jnp.bfloat16
I32 = jnp.int32

EPS = 1e-5
NEG = -1e30
LANES = 128
SUBLANES = 8
VMEM_LIMIT = 56 * 1024 * 1024

CONV_K = 3
DIFF_HEADS = 4
DIFF_QK_DIM = 64
DIFF_V_DIM = 128
MEM_HEADS = 4
MEM_HEAD_DIM = 64
N_BUCKETS = 32
MAX_DISTANCE = 128
N_EXPERTS = 32
TOP_K = 4
SWIGLU_ALPHA = 1.702
SWIGLU_LIMIT = 7.0
LAM_INIT = 0.8 - 0.6 * 1.0
LOG2E = 1.4426950408889634

SEQ_TILE = 512
ATTN_TILE = 512
ROUTE_TILE = 512
MOE_BLOCK = 256
ROW_TILE = 256


def _rms(x, g):
    return x * lax.rsqrt(jnp.mean(x * x, axis=-1, keepdims=True) + EPS) * g


def _cparams(sem):
    return pltpu.CompilerParams(dimension_semantics=sem, vmem_limit_bytes=VMEM_LIMIT)


def _mem_kv_kernel(mem_ref, g_ref, w_ref, km_ref, vm_ref):
    y = _rms(mem_ref[0], g_ref[...])
    kv = jnp.dot(y.astype(BF16), w_ref[...], preferred_element_type=F32)
    half = kv.shape[1] // 2
    km_ref[0] = kv[:, :half].astype(BF16)
    vm_ref[0] = kv[:, half:].astype(BF16)


def _mem_kv(mem, g, w):
    B, M, D = mem.shape
    W2 = w.shape[1]
    return pl.pallas_call(
        _mem_kv_kernel,
        grid=(B,),
        in_specs=[pl.BlockSpec((1, M, D), lambda b: (b, 0, 0)),
                  pl.BlockSpec((1, D), lambda b: (0, 0)),
                  pl.BlockSpec((D, W2), lambda b: (0, 0))],
        out_specs=[pl.BlockSpec((1, M, W2 // 2), lambda b: (b, 0, 0)),
                   pl.BlockSpec((1, M, W2 // 2), lambda b: (b, 0, 0))],
        out_shape=[jax.ShapeDtypeStruct((B, M, W2 // 2), BF16)] * 2,
        compiler_params=_cparams(("arbitrary",)),
        name="mem_kv",
    )(mem, g, w)


def _mix_in_kernel(x_ref, g_ref, w_ref, cw_ref, km_ref, vm_ref,
                   yc_ref, ym_ref, q_ref, k_ref, v_ref, cbuf, *, ts, cc, qk, vw, mw):
    si = pl.program_id(1)
    h = _rms(x_ref[0], g_ref[...])
    proj = jnp.dot(h.astype(BF16), w_ref[...], preferred_element_type=F32)
    u = proj[:, 0:cc]
    gb = proj[:, cc:2 * cc]
    gc = proj[:, 2 * cc:3 * cc]
    cu = gc * u

    @pl.when(si == 0)
    def _():
        cbuf[0:SUBLANES, :] = jnp.zeros((SUBLANES, cc), F32)

    cbuf[SUBLANES:SUBLANES + ts, :] = cu
    w = cw_ref[...]
    y = (w[2:3, :] * cu + w[1:2, :] * cbuf[SUBLANES - 1:SUBLANES - 1 + ts, :]
         + w[0:1, :] * cbuf[SUBLANES - 2:SUBLANES - 2 + ts, :])
    yc_ref[0] = (gb * y).astype(BF16)
    cbuf[0:SUBLANES, :] = cbuf[ts:ts + SUBLANES, :]

    o = 3 * cc
    scale = DIFF_QK_DIM ** -0.5 * LOG2E
    q_ref[0] = (proj[:, o:o + qk] * scale).astype(BF16)
    k_ref[0] = proj[:, o + qk:o + 2 * qk].astype(BF16)
    v_ref[0] = proj[:, o + 2 * qk:o + 2 * qk + vw].astype(BF16)
    o = o + 2 * qk + vw
    qm = (proj[:, o:o + mw] * (MEM_HEAD_DIM ** -0.5)).astype(BF16)

    km = km_ref[0]
    vm = vm_ref[0]
    lane = lax.broadcasted_iota(I32, (1, mw), 1)
    acc = jnp.zeros((ts, mw), F32)
    for hh in range(MEM_HEADS):
        msk = (lane >= hh * MEM_HEAD_DIM) & (lane < (hh + 1) * MEM_HEAD_DIM)
        qh = jnp.where(msk, qm, jnp.zeros_like(qm))
        s = lax.dot_general(qh, km, (((1,), (1,)), ((), ())), preferred_element_type=F32)
        p = jnp.exp(s - jnp.max(s, axis=-1, keepdims=True))
        l = jnp.sum(p, axis=-1, keepdims=True)
        oh = jnp.dot(p.astype(BF16), vm, preferred_element_type=F32)
        acc = acc + jnp.where(msk, oh / l, 0.0)
    ym_ref[0] = acc.astype(BF16)


def _mix_in(x, g, w_bf, conv_w, km, vm):
    B, S, D = x.shape
    ts = min(SEQ_TILE, S)
    cc = conv_w.shape[1]
    qk = DIFF_HEADS * 2 * DIFF_QK_DIM
    vw = DIFF_HEADS * DIFF_V_DIM
    mw = MEM_HEADS * MEM_HEAD_DIM
    M = km.shape[1]
    kern = functools.partial(_mix_in_kernel, ts=ts, cc=cc, qk=qk, vw=vw, mw=mw)
    tile = lambda n: pl.BlockSpec((1, ts, n), lambda b, s: (b, s, 0))
    const = lambda shape: pl.BlockSpec(shape, lambda b, s: (0,) * len(shape))
    return pl.pallas_call(
        kern,
        grid=(B, S // ts),
        in_specs=[tile(D), const((1, D)), const(w_bf.shape), const(conv_w.shape),
                  pl.BlockSpec((1, M, mw), lambda b, s: (b, 0, 0)),
                  pl.BlockSpec((1, M, mw), lambda b, s: (b, 0, 0))],
        out_specs=[tile(cc), tile(mw), tile(qk), tile(qk), tile(vw)],
        out_shape=[jax.ShapeDtypeStruct((B, S, n), BF16) for n in (cc, mw, qk, qk, vw)],
        scratch_shapes=[pltpu.VMEM((ts + SUBLANES, cc), F32)],
        compiler_params=_cparams(("arbitrary", "arbitrary")),
        name="mix_in",
    )(x, g, w_bf, conv_w, km, vm)


def _attn_kernel(lam_ref, qt_ref, k_ref, vt_ref, bias_ref, g_ref, o_ref, a1, a2, *, t):
    i = pl.program_id(2)
    qt = qt_ref[0, 0]
    row = lax.broadcasted_iota(I32, (LANES, 1), 0)
    zero = jnp.zeros_like(qt)
    qa = jnp.where(row < DIFF_QK_DIM, qt, zero)
    qb = jnp.where(row >= DIFF_QK_DIM, qt, zero)
    a1[...] = jnp.zeros(a1.shape, F32)
    a2[...] = jnp.zeros(a2.shape, F32)

    def step(j, bias, carry):
        start = pl.multiple_of(j * t, t)
        kb = k_ref[0, pl.ds(start, t), :]
        vt = vt_ref[0, 0, j]
        out = []
        for qx, a, (m, l) in ((qa, a1, carry[0:2]), (qb, a2, carry[2:4])):
            s = jnp.dot(kb, qx, preferred_element_type=F32)
            if bias is not None:
                s = s + bias
            m_next = jnp.maximum(m, jnp.max(s, axis=0, keepdims=True))
            p = jnp.exp2(s - m_next)
            alpha = jnp.exp2(m - m_next)
            l_next = alpha * l + jnp.sum(p, axis=0, keepdims=True)
            a[...] = a[...] * alpha + jnp.dot(vt, p.astype(BF16), preferred_element_type=F32)
            out += [m_next, l_next]
        return tuple(out)

    m0 = jnp.full((1, t), NEG, F32)
    l0 = jnp.zeros((1, t), F32)
    carry = lax.fori_loop(0, jnp.maximum(i - 1, 0), lambda j, c: step(j, None, c), (m0, l0, m0, l0))
    carry = lax.cond(i >= 1, lambda c: step(i - 1, bias_ref[0, 1], c), lambda c: c, carry)
    m1, l1, m2, l2 = step(i, bias_ref[0, 0], carry)

    lq = lam_ref[...]
    lam = (jnp.exp(jnp.sum(lq[0:1, :] * lq[1:2, :], axis=1, keepdims=True))
           - jnp.exp(jnp.sum(lq[2:3, :] * lq[3:4, :], axis=1, keepdims=True)) + LAM_INIT)
    ot = a1[...] / l1 - lam * (a2[...] / l2)
    o = _rms(ot.T, g_ref[...]) * (1.0 - LAM_INIT)
    o_ref[0] = o.astype(BF16)


def _t5_bucket(n):
    max_exact = N_BUCKETS // 2
    nf = jnp.maximum(n, 1).astype(F32)
    large = max_exact + (jnp.log(nf / max_exact) / np.log(MAX_DISTANCE / max_exact)
                         * (N_BUCKETS - max_exact)).astype(I32)
    large = jnp.minimum(large, N_BUCKETS - 1)
    return jnp.where(n < max_exact, n, large)


def _bias_tiles(rel_bias, t):
    assert t >= MAX_DISTANCE
    H = rel_bias.shape[1]
    dist = jnp.arange(2 * t, dtype=I32)
    rb = rel_bias.astype(F32)
    tab = ((rb[_t5_bucket(dist)] - rb[N_BUCKETS - 1][None, :]) * LOG2E).T

    def toeplitz(u):
        return jnp.tile(u, (1, t))[:, :t * (2 * t - 1)].reshape(H, t, 2 * t - 1)[:, :, :t]

    diag = toeplitz(jnp.concatenate([tab[:, :t], jnp.full((H, t), NEG, F32)], axis=1))
    prev = toeplitz(jnp.concatenate([tab[:, t:], tab[:, :t]], axis=1))
    return jnp.stack([diag, prev], axis=1)


def _diff_attn(q, k, v, bias, lam_qk, g):
    B, S, _ = q.shape
    t = min(ATTN_TILE, S)
    H = DIFF_HEADS
    qt = jnp.transpose(q.reshape(B, S, H, LANES), (0, 2, 3, 1))
    vt = jnp.transpose(v.reshape(B, S // t, t, H, DIFF_V_DIM), (0, 3, 1, 4, 2))
    kern = functools.partial(_attn_kernel, t=t)
    return pl.pallas_call(
        kern,
        grid=(B, H, S // t),
        in_specs=[pl.BlockSpec(lam_qk.shape, lambda b, h, i: (0, 0)),
                  pl.BlockSpec((1, 1, LANES, t), lambda b, h, i: (b, h, 0, i)),
                  pl.BlockSpec((1, S, LANES), lambda b, h, i: (b, 0, h)),
                  pl.BlockSpec((1, 1, S // t, DIFF_V_DIM, t), lambda b, h, i: (b, h, 0, 0, 0)),
                  pl.BlockSpec((1, 2, t, t), lambda b, h, i: (h, 0, 0, 0)),
                  pl.BlockSpec((1, DIFF_V_DIM), lambda b, h, i: (0, 0))],
        out_specs=pl.BlockSpec((1, t, LANES), lambda b, h, i: (b, i, h)),
        out_shape=jax.ShapeDtypeStruct((B, S, H * DIFF_V_DIM), BF16),
        scratch_shapes=[pltpu.VMEM((DIFF_V_DIM, t), F32)] * 2,
        compiler_params=_cparams(("arbitrary", "arbitrary", "arbitrary")),
        name="diff_attn",
    )(lam_qk, qt, k, vt, bias, g)


def _mix_out_kernel(x_ref, yc_ref, yd_ref, ym_ref, w_ref, g_ref, rw_ref, rb_ref,
                    x1_ref, h_ref, idx_ref, gate_ref, *, cc, dw):
    x1 = (x_ref[...]
          + jnp.dot(yc_ref[...], w_ref[0:cc, :], preferred_element_type=F32)
          + jnp.dot(yd_ref[...], w_ref[cc:cc + dw, :], preferred_element_type=F32)
          + jnp.dot(ym_ref[...], w_ref[cc + dw:, :], preferred_element_type=F32))
    x1_ref[...] = x1
    h = _rms(x1, g_ref[...])
    h_ref[...] = h
    logits = jnp.dot(h, rw_ref[...], preferred_element_type=F32,
                     precision=lax.Precision.HIGHEST) + rb_ref[...]
    lane = lax.broadcasted_iota(I32, logits.shape, 1).astype(F32)
    vals, idxs = [], []
    cur = logits
    for _ in range(TOP_K):
        mk = jnp.max(cur, axis=1, keepdims=True)
        ik = jnp.min(jnp.where(cur == mk, lane, float(LANES)), axis=1, keepdims=True)
        vals.append(mk)
        idxs.append(ik)
        cur = jnp.where(lane == ik, -jnp.inf, cur)
    es = [jnp.exp(v - vals[0]) for v in vals]
    den = es[0] + es[1] + es[2] + es[3]
    idx_out = jnp.zeros(logits.shape, F32)
    gate_out = jnp.zeros(logits.shape, F32)
    for kk in range(TOP_K):
        idx_out = jnp.where(lane == float(kk), idxs[kk], idx_out)
        gate_out = jnp.where(lane == float(kk), es[kk] / den, gate_out)
    idx_ref[...] = idx_out.astype(I32)
    gate_ref[...] = gate_out


def _mix_out(x2, yc, yd, ym, w_bf, g, rw_pad, rb_pad):
    T, D = x2.shape
    tm = min(SEQ_TILE, T)
    cc, dw, mw = yc.shape[1], yd.shape[1], ym.shape[1]
    kern = functools.partial(_mix_out_kernel, cc=cc, dw=dw)
    tile = lambda n: pl.BlockSpec((tm, n), lambda i: (i, 0))
    const = lambda shape: pl.BlockSpec(shape, lambda i: (0, 0))
    return pl.pallas_call(
        kern,
        grid=(T // tm,),
        in_specs=[tile(D), tile(cc), tile(dw), tile(mw), const(w_bf.shape), const((1, D)),
                  const(rw_pad.shape), const(rb_pad.shape)],
        out_specs=[tile(D), tile(D), tile(LANES), tile(LANES)],
        out_shape=[jax.ShapeDtypeStruct((T, D), F32), jax.ShapeDtypeStruct((T, D), F32),
                   jax.ShapeDtypeStruct((T, LANES), I32), jax.ShapeDtypeStruct((T, LANES), F32)],
        compiler_params=_cparams(("arbitrary",)),
        name="mix_out",
    )(x2, yc, yd, ym, w_bf, g, rw_pad, rb_pad)


def _route_kernel(idx_ref, pos_ref, meta_ref, cnt, pstart, *, tile, blk):
    ph = pl.program_id(0)
    i = pl.program_id(1)
    lane = lax.broadcasted_iota(I32, (tile, LANES), 1)
    idx = idx_ref[...]
    ohs = [(idx[:, kk:kk + 1] == lane) for kk in range(TOP_K)]
    oh = jnp.zeros((tile, LANES), F32)
    for m in ohs:
        oh = oh + jnp.where(m, 1.0, 0.0)
    colsum = jnp.sum(oh, axis=0, keepdims=True)

    @pl.when((ph == 0) & (i == 0))
    def _():
        cnt[...] = jnp.zeros(cnt.shape, F32)

    @pl.when(ph == 0)
    def _():
        cnt[...] = cnt[...] + colsum

    @pl.when((ph == 1) & (i == 0))
    def _():
        total = cnt[...]
        padded = jnp.floor((total + (blk - 1)) * (1.0 / blk)) * blk
        l8 = lax.broadcasted_iota(I32, total.shape, 1)
        end = padded
        sh = 1
        while sh < LANES:
            end = end + jnp.where(l8 >= sh, pltpu.roll(end, sh, axis=1), 0.0)
            sh *= 2
        pstart[...] = end - padded
        row = lax.broadcasted_iota(I32, total.shape, 0)
        meta_ref[...] = jnp.where(row == 0, total, jnp.where(row == 1, end - padded, end))
        cnt[...] = jnp.zeros(cnt.shape, F32)

    @pl.when(ph == 1)
    def _():
        r = lax.broadcasted_iota(I32, (tile, tile), 0)
        c = lax.broadcasted_iota(I32, (tile, tile), 1)
        tri = jnp.where(c < r, 1.0, 0.0).astype(BF16)
        before = jnp.dot(tri, oh.astype(BF16), preferred_element_type=F32)
        base = before + cnt[0:1, :] + pstart[0:1, :]
        out = jnp.zeros((tile, LANES), F32)
        for kk in range(TOP_K):
            pk = jnp.sum(jnp.where(ohs[kk], base, 0.0), axis=1, keepdims=True)
            out = jnp.where(lane == kk, pk, out)
        pos_ref[...] = out.astype(I32)
        cnt[...] = cnt[...] + colsum


def _route(idx, blk):
    T = idx.shape[0]
    tile = min(ROUTE_TILE, T)
    kern = functools.partial(_route_kernel, tile=tile, blk=blk)
    return pl.pallas_call(
        kern,
        grid=(2, T // tile),
        in_specs=[pl.BlockSpec((tile, LANES), lambda p, i: (i, 0))],
        out_specs=[pl.BlockSpec((tile, LANES), lambda p, i: (p * i, 0)),
                   pl.BlockSpec((SUBLANES, LANES), lambda p, i: (0, 0))],
        out_shape=[jax.ShapeDtypeStruct((T, LANES), I32),
                   jax.ShapeDtypeStruct((SUBLANES, LANES), F32)],
        scratch_shapes=[pltpu.VMEM((SUBLANES, LANES), F32), pltpu.VMEM((SUBLANES, LANES), F32)],
        compiler_params=_cparams(("arbitrary", "arbitrary")),
        name="route",
    )(idx)


def _row_copy(src, si, dst, di, sem):
    return pltpu.make_async_copy(src.at[pl.ds(si, 1), :], dst.at[pl.ds(di, 1), :], sem)


def _dispatch_kernel(cnt_ref, pst_ref, dest_ref, h_ref, xs_ref, zbuf, sem, *, tm, blk):
    def issue(r, c):
        for kk in range(TOP_K):
            _row_copy(h_ref, r, xs_ref, dest_ref[r * TOP_K + kk], sem).start()
        return c

    lax.fori_loop(0, tm, issue, 0)

    def drain(r, c):
        for kk in range(TOP_K):
            _row_copy(h_ref, 0, xs_ref, 0, sem).wait()
        return c

    lax.fori_loop(0, tm, drain, 0)

    @pl.when(pl.program_id(0) == pl.num_programs(0) - 1)
    def _():
        zbuf[...] = jnp.zeros(zbuf.shape, F32)

        def per_expert(e, c):
            n = cnt_ref[e]
            base = pst_ref[e]
            n_pad = lax.shift_right_logical(n + (blk - 1), int(np.log2(blk))) * blk

            def fill(r, c2):
                cp = _row_copy(zbuf, 0, xs_ref, base + r, sem)
                cp.start()
                cp.wait()
                return c2

            lax.fori_loop(n, n_pad, fill, 0)
            return c

        lax.fori_loop(0, N_EXPERTS, per_expert, 0)


def _dispatch(counts, pstart, dest, h, n_pad, blk):
    T, D = h.shape
    tm = min(ROW_TILE, T)
    kern = functools.partial(_dispatch_kernel, tm=tm, blk=blk)
    return pl.pallas_call(
        kern,
        grid_spec=pltpu.PrefetchScalarGridSpec(
            num_scalar_prefetch=2,
            grid=(T // tm,),
            in_specs=[pl.BlockSpec((tm * TOP_K,), lambda i, *_: (i,), memory_space=pltpu.SMEM),
                      pl.BlockSpec((tm, D), lambda i, *_: (i, 0))],
            out_specs=pl.BlockSpec(memory_space=pl.ANY),
            scratch_shapes=[pltpu.VMEM((SUBLANES, D), F32), pltpu.SemaphoreType.DMA(())]),
        out_shape=jax.ShapeDtypeStruct((n_pad, D), F32),
        compiler_params=_cparams(("arbitrary",)),
        name="dispatch",
    )(counts, pstart, dest, h)


def _expert_kernel(be_ref, nv_ref, x_ref, wgu_ref, bgu_ref, wd_ref, bd_ref, y_ref, *, de):
    @pl.when(pl.program_id(0) < nv_ref[0])
    def _():
        gu = jnp.dot(x_ref[...].astype(BF16), wgu_ref[0], preferred_element_type=F32) + bgu_ref[0]
        gate = jnp.minimum(gu[:, :de], SWIGLU_LIMIT)
        up = jnp.clip(gu[:, de:], -SWIGLU_LIMIT, SWIGLU_LIMIT)
        act = (up + 1.0) * (gate * jax.nn.sigmoid(SWIGLU_ALPHA * gate))
        y_ref[...] = jnp.dot(act.astype(BF16), wd_ref[0], preferred_element_type=F32) + bd_ref[0]


def _expert_ffn(block_e, nvalid, xs, wgu, bgu, wd, bd, blk):
    n_pad, D = xs.shape
    de = wd.shape[1]
    kern = functools.partial(_expert_kernel, de=de)
    row = lambda j, be, nv: (jnp.minimum(j, nv[0] - 1), 0)
    exp = lambda j, be, nv: (be[jnp.minimum(j, nv[0] - 1)], 0, 0)
    return pl.pallas_call(
        kern,
        grid_spec=pltpu.PrefetchScalarGridSpec(
            num_scalar_prefetch=2,
            grid=(n_pad // blk,),
            in_specs=[pl.BlockSpec((blk, D), row),
                      pl.BlockSpec((1, D, 2 * de), exp),
                      pl.BlockSpec((1, 1, 2 * de), exp),
                      pl.BlockSpec((1, de, D), exp),
                      pl.BlockSpec((1, 1, D), exp)],
            out_specs=pl.BlockSpec((blk, D), row)),
        out_shape=jax.ShapeDtypeStruct((n_pad, D), F32),
        compiler_params=_cparams(("arbitrary",)),
        name="expert_ffn",
    )(block_e, nvalid, xs, wgu, bgu, wd, bd)


def _combine_kernel(dest_ref, x1_ref, gate_ref, g_ref, y_ref, o_ref, ybuf, sem, *, tm):
    def issue(r, c):
        for kk in range(TOP_K):
            _row_copy(y_ref, dest_ref[r * TOP_K + kk], ybuf.at[kk], r, sem).start()
        return c

    lax.fori_loop(0, tm, issue, 0)

    def drain(r, c):
        for kk in range(TOP_K):
            _row_copy(y_ref, 0, ybuf.at[kk], 0, sem).wait()
        return c

    lax.fori_loop(0, tm, drain, 0)

    gates = gate_ref[...]
    acc = x1_ref[...]
    for kk in range(TOP_K):
        acc = acc + gates[:, kk:kk + 1] * ybuf[kk]
    o_ref[...] = _rms(acc, g_ref[...])


def _combine(dest, x1, gates, g, y):
    T, D = x1.shape
    tm = min(ROW_TILE, T)
    kern = functools.partial(_combine_kernel, tm=tm)
    return pl.pallas_call(
        kern,
        grid=(T // tm,),
        in_specs=[pl.BlockSpec((tm * TOP_K,), lambda i: (i,), memory_space=pltpu.SMEM),
                  pl.BlockSpec((tm, D), lambda i: (i, 0)),
                  pl.BlockSpec((tm, LANES), lambda i: (i, 0)),
                  pl.BlockSpec((1, D), lambda i: (0, 0)),
                  pl.BlockSpec(memory_space=pl.ANY)],
        out_specs=pl.BlockSpec((tm, D), lambda i: (i, 0)),
        out_shape=jax.ShapeDtypeStruct((T, D), F32),
        scratch_shapes=[pltpu.VMEM((TOP_K, tm, D), F32), pltpu.SemaphoreType.DMA(())],
        compiler_params=_cparams(("arbitrary",)),
        name="combine",
    )(dest, x1, gates, g, y)


def kernel(x, mem, rel_bias, attn_norm_g, w_mix_in, conv_w, lambda_qk, diff_subln_g, mem_norm_g,
           w_mem_kv, w_mix_out, ffn_norm_g, router_w, router_b, w_gate_up, b_gate_up, w_down,
           b_down, final_norm_g):
    B, S, D = x.shape
    T = B * S
    E = router_w.shape[-1]
    assert w_mix_in.shape[0] == 1 and E == N_EXPERTS

    km, vm = _mem_kv(mem, mem_norm_g[0][None, :], w_mem_kv[0].astype(BF16))
    yc, ym, q, k, v = _mix_in(x, attn_norm_g[0][None, :], w_mix_in[0].astype(BF16), conv_w[0], km, vm)

    t = min(ATTN_TILE, S)
    bias = _bias_tiles(rel_bias, t)
    yd = _diff_attn(q, k, v, bias, lambda_qk[0], diff_subln_g[0][None, :])

    rw_pad = jnp.zeros((D, LANES), F32).at[:, :E].set(router_w[0])
    rb_pad = jnp.full((1, LANES), NEG, F32).at[0, :E].set(router_b[0])
    x1, h, idx, gates = _mix_out(x.reshape(T, D), yc.reshape(T, -1), yd.reshape(T, -1),
                                 ym.reshape(T, -1), w_mix_out[0].astype(BF16),
                                 ffn_norm_g[0][None, :], rw_pad, rb_pad)

    blk = MOE_BLOCK
    n_blocks = (T * TOP_K) // blk + E
    n_pad = n_blocks * blk
    pos, meta = _route(idx, blk)
    dest = pos[:, :TOP_K].reshape(T * TOP_K)
    counts = meta[0, :E].astype(I32)
    pstart = meta[1, :E].astype(I32)
    pend = meta[2, :E].astype(I32)
    block_start = jnp.arange(n_blocks, dtype=I32) * blk
    block_e = jnp.minimum(jnp.sum(pend[None, :] <= block_start[:, None], axis=1), E - 1).astype(I32)
    nvalid = (pend[E - 1:E] // blk).astype(I32)

    xs = _dispatch(counts, pstart, dest, h, n_pad, blk)
    y = _expert_ffn(block_e, nvalid, xs, w_gate_up[0].astype(BF16), b_gate_up[0][:, None, :],
                    w_down[0].astype(BF16), b_down[0][:, None, :], blk)
    out = _combine(dest, x1, gates, final_norm_g[None, :], y)
    return out.reshape(B, S, D)
```

```python
import functools

import numpy as np
import jax
import jax.numpy as jnp
from jax import lax
from jax.experimental import pallas as pl
from jax.experimental.pallas import tpu as pltpu

F32 = jnp.float32
BF16 = jnp.bfloat16
I32 = jnp.int32

EPS = 1e-5
NEG = -1e30
LANES = 128
SUBLANES = 8
VMEM_LIMIT = 56 * 1024 * 1024

CONV_K = 3
DIFF_HEADS = 4
DIFF_QK_DIM = 64
DIFF_V_DIM = 128
MEM_HEADS = 4
MEM_HEAD_DIM = 64
N_BUCKETS = 32
MAX_DISTANCE = 128
N_EXPERTS = 32
TOP_K = 4
SWIGLU_ALPHA = 1.702
SWIGLU_LIMIT = 7.0
LAM_INIT = 0.8 - 0.6 * 1.0
LOG2E = 1.4426950408889634

SEQ_TILE = 512
ATTN_TILE = 1024
ROUTE_TILE = 512
MOE_BLOCK = 512
ROW_TILE = 256
DMA_UNROLL = 8


def _rms(x, g):
    return x * lax.rsqrt(jnp.mean(x * x, axis=-1, keepdims=True) + EPS) * g


def _cparams(sem):
    return pltpu.CompilerParams(dimension_semantics=sem, vmem_limit_bytes=VMEM_LIMIT)


def _to_slot_tiles(ref, x):
    n, d = x.shape
    c = d // LANES
    for s in range(c):
        ref[pl.ds(s, n, stride=c), :] = x[:, s * LANES:(s + 1) * LANES]


def _from_slot_tiles(ref, n, d):
    c = d // LANES
    return jnp.concatenate([ref[pl.ds(s, n, stride=c), :] for s in range(c)], axis=1)


def _slot_rows(i, c):
    return pl.ds(i * c, c) if isinstance(i, int) else pl.ds(pl.multiple_of(i * c, c), c)


def _slot_copy(src, si, dst, di, sem, c):
    return pltpu.make_async_copy(src.at[_slot_rows(si, c), :], dst.at[_slot_rows(di, c), :], sem)


def _mem_kv_kernel(mem_ref, g_ref, w_ref, km_ref, vm_ref):
    y = _rms(mem_ref[0], g_ref[...])
    kv = jnp.dot(y.astype(BF16), w_ref[...], preferred_element_type=F32)
    half = kv.shape[1] // 2
    km_ref[0] = kv[:, :half].astype(BF16)
    vm_ref[0] = kv[:, half:].astype(BF16)


def _mem_kv(mem, g, w):
    B, M, D = mem.shape
    W2 = w.shape[1]
    return pl.pallas_call(
        _mem_kv_kernel,
        grid=(B,),
        in_specs=[pl.BlockSpec((1, M, D), lambda b: (b, 0, 0)),
                  pl.BlockSpec((1, D), lambda b: (0, 0)),
                  pl.BlockSpec((D, W2), lambda b: (0, 0))],
        out_specs=[pl.BlockSpec((1, M, W2 // 2), lambda b: (b, 0, 0)),
                   pl.BlockSpec((1, M, W2 // 2), lambda b: (b, 0, 0))],
        out_shape=[jax.ShapeDtypeStruct((B, M, W2 // 2), BF16)] * 2,
        compiler_params=_cparams(("arbitrary",)),
        name="mem_kv",
    )(mem, g, w)


def _mix_in_kernel(x_ref, g_ref, w_ref, cw_ref, km_ref, vm_ref,
                   yc_ref, ym_ref, qt_ref, k_ref, vt_ref, cbuf, *, ts, cc, qk, vw, mw):
    si = pl.program_id(1)
    h = _rms(x_ref[0], g_ref[...])
    proj = jnp.dot(h.astype(BF16), w_ref[...], preferred_element_type=F32)
    u = proj[:, 0:cc]
    gb = proj[:, cc:2 * cc]
    gc = proj[:, 2 * cc:3 * cc]
    cu = gc * u

    @pl.when(si == 0)
    def _():
        cbuf[0:SUBLANES, :] = jnp.zeros((SUBLANES, cc), F32)

    cbuf[SUBLANES:SUBLANES + ts, :] = cu
    w = cw_ref[...]
    y = (w[2:3, :] * cu + w[1:2, :] * cbuf[SUBLANES - 1:SUBLANES - 1 + ts, :]
         + w[0:1, :] * cbuf[SUBLANES - 2:SUBLANES - 2 + ts, :])
    yc_ref[0] = (gb * y).astype(BF16)
    cbuf[0:SUBLANES, :] = cbuf[ts:ts + SUBLANES, :]

    o = 3 * cc
    scale = DIFF_QK_DIM ** -0.5 * LOG2E
    for hh in range(DIFF_HEADS):
        qh = proj[:, o + hh * LANES:o + (hh + 1) * LANES] * scale
        qt_ref[0, hh] = qh.T.astype(BF16)
        vh = proj[:, o + 2 * qk + hh * DIFF_V_DIM:o + 2 * qk + (hh + 1) * DIFF_V_DIM]
        vt_ref[0, hh, 0] = vh.T.astype(BF16)
    k_ref[0] = proj[:, o + qk:o + 2 * qk].astype(BF16)
    o = o + 2 * qk + vw
    qm = (proj[:, o:o + mw] * (MEM_HEAD_DIM ** -0.5)).astype(BF16)

    km = km_ref[0]
    vm = vm_ref[0]
    lane = lax.broadcasted_iota(I32, (1, mw), 1)
    acc = jnp.zeros((ts, mw), F32)
    for hh in range(MEM_HEADS):
        msk = (lane >= hh * MEM_HEAD_DIM) & (lane < (hh + 1) * MEM_HEAD_DIM)
        qh = jnp.where(msk, qm, jnp.zeros_like(qm))
        s = lax.dot_general(qh, km, (((1,), (1,)), ((), ())), preferred_element_type=F32)
        p = jnp.exp(s - jnp.max(s, axis=-1, keepdims=True))
        l = jnp.sum(p, axis=-1, keepdims=True)
        oh = jnp.dot(p.astype(BF16), vm, preferred_element_type=F32)
        acc = acc + jnp.where(msk, oh / l, 0.0)
    ym_ref[0] = acc.astype(BF16)


def _mix_in(x, g, w_bf, conv_w, km, vm, t):
    B, S, D = x.shape
    ts = min(SEQ_TILE, S)
    assert t % ts == 0
    r = t // ts
    H = DIFF_HEADS
    cc = conv_w.shape[1]
    qk = H * 2 * DIFF_QK_DIM
    vw = H * DIFF_V_DIM
    mw = MEM_HEADS * MEM_HEAD_DIM
    M = km.shape[1]
    kern = functools.partial(_mix_in_kernel, ts=ts, cc=cc, qk=qk, vw=vw, mw=mw)
    tile = lambda n: pl.BlockSpec((1, ts, n), lambda b, s: (b, s, 0))
    const = lambda shape: pl.BlockSpec(shape, lambda b, s: (0,) * len(shape))
    return pl.pallas_call(
        kern,
        grid=(B, S // ts),
        in_specs=[tile(D), const((1, D)), const(w_bf.shape), const(conv_w.shape),
                  pl.BlockSpec((1, M, mw), lambda b, s: (b, 0, 0)),
                  pl.BlockSpec((1, M, mw), lambda b, s: (b, 0, 0))],
        out_specs=[tile(cc), tile(mw),
                   pl.BlockSpec((1, H, 2 * DIFF_QK_DIM, ts), lambda b, s: (b, 0, 0, s)),
                   tile(qk),
                   pl.BlockSpec((1, H, 1, DIFF_V_DIM, ts), lambda b, s: (b, 0, s // r, 0, s % r))],
        out_shape=[jax.ShapeDtypeStruct((B, S, cc), BF16), jax.ShapeDtypeStruct((B, S, mw), BF16),
                   jax.ShapeDtypeStruct((B, H, 2 * DIFF_QK_DIM, S), BF16),
                   jax.ShapeDtypeStruct((B, S, qk), BF16),
                   jax.ShapeDtypeStruct((B, H, S // t, DIFF_V_DIM, t), BF16)],
        scratch_shapes=[pltpu.VMEM((ts + SUBLANES, cc), F32)],
        compiler_params=_cparams(("arbitrary", "arbitrary")),
        name="mix_in",
    )(x, g, w_bf, conv_w, km, vm)


def _attn_kernel(lam_ref, qt_ref, k_ref, vt_ref, bias_ref, g_ref, o_ref, a1, a2, s1, s2, *, t):
    i = pl.program_id(2)
    qt = qt_ref[0, 0]
    row = lax.broadcasted_iota(I32, (LANES, 1), 0)
    zero = jnp.zeros_like(qt)
    qa = jnp.where(row < DIFF_QK_DIM, qt, zero)
    qb = jnp.where(row >= DIFF_QK_DIM, qt, zero)
    a1[...] = jnp.zeros(a1.shape, F32)
    a2[...] = jnp.zeros(a2.shape, F32)

    def score(j, qx, s):
        start = pl.multiple_of(j * t, t)
        s[...] = jnp.dot(k_ref[0, pl.ds(start, t), :], qx, preferred_element_type=F32)

    def absorb(j, s, a, m, l, bias):
        sc = s[...]
        if bias is not None:
            sc = sc + bias
        m_next = jnp.maximum(m, jnp.max(sc, axis=0, keepdims=True))
        p = jnp.exp2(sc - m_next)
        alpha = jnp.exp2(m - m_next)
        l_next = alpha * l + jnp.sum(p, axis=0, keepdims=True)
        a[...] = a[...] * alpha + jnp.dot(vt_ref[0, 0, j], p.astype(BF16), preferred_element_type=F32)
        return m_next, l_next

    def far_step(j, carry):
        m1, l1, m2, l2 = carry
        score(j, qb, s2)
        m1, l1 = absorb(j, s1, a1, m1, l1, None)
        score(j + 1, qa, s1)
        m2, l2 = absorb(j, s2, a2, m2, l2, None)
        return m1, l1, m2, l2

    def prev_step(carry):
        m1, l1, m2, l2 = carry
        score(i - 1, qb, s2)
        m1, l1 = absorb(i - 1, s1, a1, m1, l1, bias_prev)
        m2, l2 = absorb(i - 1, s2, a2, m2, l2, bias_prev)
        return m1, l1, m2, l2

    hb = t // 2
    near = bias_ref[0, 0]
    far = bias_ref[0, 1]
    zeros = jnp.zeros((hb, hb), F32)
    bias_prev = jnp.concatenate([jnp.zeros((hb, t), F32),
                                 jnp.concatenate([far, zeros], axis=1)], axis=0)

    def diag_step(carry):
        start = pl.multiple_of(i * t, t)
        k_lo = k_ref[0, pl.ds(start, hb), :]
        k_hi = k_ref[0, pl.ds(start + hb, hb), :]
        vt = vt_ref[0, 0, i]
        bias_lo = jnp.concatenate([near, far], axis=1)
        out = []
        for qx, a, (m, l) in ((qa, a1, carry[0:2]), (qb, a2, carry[2:4])):
            s_lo = jnp.dot(k_lo, qx, preferred_element_type=F32) + bias_lo
            s_hi = jnp.dot(k_hi, qx[:, hb:], preferred_element_type=F32) + near
            m_lo = jnp.maximum(m, jnp.max(s_lo, axis=0, keepdims=True))
            m_next = jnp.concatenate(
                [m_lo[:, :hb], jnp.maximum(m_lo[:, hb:], jnp.max(s_hi, axis=0, keepdims=True))], axis=1)
            p_lo = jnp.exp2(s_lo - m_next)
            p_hi = jnp.exp2(s_hi - m_next[:, hb:])
            alpha = jnp.exp2(m - m_next)
            l_hi = jnp.concatenate([jnp.zeros((1, hb), F32), jnp.sum(p_hi, axis=0, keepdims=True)], axis=1)
            l_next = alpha * l + jnp.sum(p_lo, axis=0, keepdims=True) + l_hi
            a[...] = a[...] * alpha + jnp.dot(vt[:, :hb], p_lo.astype(BF16), preferred_element_type=F32)
            a[:, hb:] = a[:, hb:] + jnp.dot(vt[:, hb:], p_hi.astype(BF16), preferred_element_type=F32)
            out += [m_next, l_next]
        return tuple(out)

    m0 = jnp.full((1, t), NEG, F32)
    l0 = jnp.zeros((1, t), F32)
    @pl.when(i >= 1)
    def _():
        score(0, qa, s1)

    carry = lax.fori_loop(0, jnp.maximum(i - 1, 0), far_step, (m0, l0, m0, l0))
    carry = lax.cond(i >= 1, prev_step, lambda c: c, carry)
    m1, l1, m2, l2 = diag_step(carry)

    lq = lam_ref[...]
    lam = (jnp.exp(jnp.sum(lq[0:1, :] * lq[1:2, :], axis=1, keepdims=True))
           - jnp.exp(jnp.sum(lq[2:3, :] * lq[3:4, :], axis=1, keepdims=True)) + LAM_INIT)
    ot = a1[...] / l1 - lam * (a2[...] / l2)
    o = _rms(ot.T, g_ref[...]) * (1.0 - LAM_INIT)
    o_ref[0] = o.astype(BF16)


def _t5_bucket(n):
    max_exact = N_BUCKETS // 2
    nf = jnp.maximum(n, 1).astype(F32)
    large = max_exact + (jnp.log(nf / max_exact) / np.log(MAX_DISTANCE / max_exact)
                         * (N_BUCKETS - max_exact)).astype(I32)
    large = jnp.minimum(large, N_BUCKETS - 1)
    return jnp.where(n < max_exact, n, large)


def _bias_tiles(rel_bias, t):
    assert t >= MAX_DISTANCE
    H = rel_bias.shape[1]
    dist = jnp.arange(2 * t, dtype=I32)
    rb = rel_bias.astype(F32)
    tab = ((rb[_t5_bucket(dist)] - rb[N_BUCKETS - 1][None, :]) * LOG2E).T

    def toeplitz(u):
        return jnp.tile(u, (1, t))[:, :t * (2 * t - 1)].reshape(H, t, 2 * t - 1)[:, :, :t]

    diag = toeplitz(jnp.concatenate([tab[:, :t], jnp.full((H, t), NEG, F32)], axis=1))
    prev = toeplitz(jnp.concatenate([tab[:, t:], tab[:, :t]], axis=1))
    return jnp.stack([diag, prev], axis=1)


def _diff_attn(qt, k, vt, bias, lam_qk, g, t):
    B, S, _ = k.shape
    H = DIFF_HEADS
    kern = functools.partial(_attn_kernel, t=t)
    return pl.pallas_call(
        kern,
        grid=(B, H, S // t),
        in_specs=[pl.BlockSpec(lam_qk.shape, lambda b, h, i: (0, 0)),
                  pl.BlockSpec((1, 1, LANES, t), lambda b, h, i: (b, h, 0, i)),
                  pl.BlockSpec((1, S, LANES), lambda b, h, i: (b, 0, h)),
                  pl.BlockSpec((1, 1, S // t, DIFF_V_DIM, t), lambda b, h, i: (b, h, 0, 0, 0)),
                  pl.BlockSpec((1, 2, t // 2, t // 2), lambda b, h, i: (h, 0, 0, 0)),
                  pl.BlockSpec((1, DIFF_V_DIM), lambda b, h, i: (0, 0))],
        out_specs=pl.BlockSpec((1, t, LANES), lambda b, h, i: (b, i, h)),
        out_shape=jax.ShapeDtypeStruct((B, S, H * DIFF_V_DIM), BF16),
        scratch_shapes=[pltpu.VMEM((DIFF_V_DIM, t), F32)] * 2 + [pltpu.VMEM((t, t), F32)] * 2,
        compiler_params=_cparams(("arbitrary", "arbitrary", "arbitrary")),
        name="diff_attn",
    )(lam_qk, qt, k, vt, bias, g)


def _mix_out_kernel(x_ref, yc_ref, yd_ref, ym_ref, w_ref, g_ref, rw_ref, rb_ref,
                    x1_ref, h_ref, idx_ref, gate_ref, *, cc, dw):
    x1 = (x_ref[...]
          + jnp.dot(yc_ref[...], w_ref[0:cc, :], preferred_element_type=F32)
          + jnp.dot(yd_ref[...], w_ref[cc:cc + dw, :], preferred_element_type=F32)
          + jnp.dot(ym_ref[...], w_ref[cc + dw:, :], preferred_element_type=F32))
    x1_ref[...] = x1
    h = _rms(x1, g_ref[...])
    _to_slot_tiles(h_ref, h)
    h_hi = h.astype(BF16)
    h_lo = (h - h_hi.astype(F32)).astype(BF16)
    t_hi = jnp.dot(h_hi, rw_ref[...], preferred_element_type=F32)
    t_lo = jnp.dot(h_lo, rw_ref[:, :LANES], preferred_element_type=F32)
    logits = t_hi[:, :LANES] + t_hi[:, LANES:] + t_lo + rb_ref[...]
    lane = lax.broadcasted_iota(I32, logits.shape, 1).astype(F32)
    vals, idxs = [], []
    cur = logits
    for _ in range(TOP_K):
        mk = jnp.max(cur, axis=1, keepdims=True)
        ik = jnp.min(jnp.where(cur == mk, lane, float(LANES)), axis=1, keepdims=True)
        vals.append(mk)
        idxs.append(ik)
        cur = jnp.where(lane == ik, -jnp.inf, cur)
    es = [jnp.exp(v - vals[0]) for v in vals]
    den = es[0] + es[1] + es[2] + es[3]
    idx_out = jnp.zeros(logits.shape, F32)
    gate_out = jnp.zeros(logits.shape, F32)
    for kk in range(TOP_K):
        idx_out = jnp.where(lane == float(kk), idxs[kk], idx_out)
        gate_out = jnp.where(lane == float(kk), es[kk] / den, gate_out)
    idx_ref[...] = idx_out.astype(I32)
    gate_ref[...] = gate_out


def _mix_out(x2, yc, yd, ym, w_bf, g, rw_pad, rb_pad):
    T, D = x2.shape
    tm = min(SEQ_TILE, T)
    cc, dw, mw = yc.shape[1], yd.shape[1], ym.shape[1]
    kern = functools.partial(_mix_out_kernel, cc=cc, dw=dw)
    tile = lambda n: pl.BlockSpec((tm, n), lambda i: (i, 0))
    const = lambda shape: pl.BlockSpec(shape, lambda i: (0, 0))
    return pl.pallas_call(
        kern,
        grid=(T // tm,),
        in_specs=[tile(D), tile(cc), tile(dw), tile(mw), const(w_bf.shape), const((1, D)),
                  const(rw_pad.shape), const(rb_pad.shape)],
        out_specs=[tile(D), pl.BlockSpec((tm * (D // LANES), LANES), lambda i: (i, 0)),
                   tile(LANES), tile(LANES)],
        out_shape=[jax.ShapeDtypeStruct((T, D), F32),
                   jax.ShapeDtypeStruct((T * (D // LANES), LANES), F32),
                   jax.ShapeDtypeStruct((T, LANES), I32), jax.ShapeDtypeStruct((T, LANES), F32)],
        compiler_params=_cparams(("arbitrary",)),
        name="mix_out",
    )(x2, yc, yd, ym, w_bf, g, rw_pad, rb_pad)


def _route_kernel(idx_ref, pos_ref, meta_ref, cnt, pstart, *, tile, blk):
    ph = pl.program_id(0)
    i = pl.program_id(1)
    lane = lax.broadcasted_iota(I32, (tile, LANES), 1)
    idx = idx_ref[...]
    ohs = [(idx[:, kk:kk + 1] == lane) for kk in range(TOP_K)]
    oh = jnp.zeros((tile, LANES), F32)
    for m in ohs:
        oh = oh + jnp.where(m, 1.0, 0.0)
    colsum = jnp.sum(oh, axis=0, keepdims=True)

    @pl.when((ph == 0) & (i == 0))
    def _():
        cnt[...] = jnp.zeros(cnt.shape, F32)

    @pl.when(ph == 0)
    def _():
        cnt[...] = cnt[...] + colsum

    @pl.when((ph == 1) & (i == 0))
    def _():
        total = cnt[...]
        padded = jnp.floor((total + (blk - 1)) * (1.0 / blk)) * blk
        l8 = lax.broadcasted_iota(I32, total.shape, 1)
        end = padded
        sh = 1
        while sh < LANES:
            end = end + jnp.where(l8 >= sh, pltpu.roll(end, sh, axis=1), 0.0)
            sh *= 2
        pstart[...] = end - padded
        row = lax.broadcasted_iota(I32, total.shape, 0)
        meta_ref[...] = jnp.where(row == 0, total, jnp.where(row == 1, end - padded, end))
        cnt[...] = jnp.zeros(cnt.shape, F32)

    @pl.when(ph == 1)
    def _():
        r = lax.broadcasted_iota(I32, (tile, tile), 0)
        c = lax.broadcasted_iota(I32, (tile, tile), 1)
        tri = jnp.where(c < r, 1.0, 0.0).astype(BF16)
        before = jnp.dot(tri, oh.astype(BF16), preferred_element_type=F32)
        base = before + cnt[0:1, :] + pstart[0:1, :]
        out = jnp.zeros((tile, LANES), F32)
        for kk in range(TOP_K):
            pk = jnp.sum(jnp.where(ohs[kk], base, 0.0), axis=1, keepdims=True)
            out = jnp.where(lane == kk, pk, out)
        pos_ref[...] = out.astype(I32)
        cnt[...] = cnt[...] + colsum


def _route(idx, blk):
    T = idx.shape[0]
    tile = min(ROUTE_TILE, T)
    kern = functools.partial(_route_kernel, tile=tile, blk=blk)
    return pl.pallas_call(
        kern,
        grid=(2, T // tile),
        in_specs=[pl.BlockSpec((tile, LANES), lambda p, i: (i, 0))],
        out_specs=[pl.BlockSpec((tile, LANES), lambda p, i: (p * i, 0)),
                   pl.BlockSpec((SUBLANES, LANES), lambda p, i: (0, 0))],
        out_shape=[jax.ShapeDtypeStruct((T, LANES), I32),
                   jax.ShapeDtypeStruct((SUBLANES, LANES), F32)],
        scratch_shapes=[pltpu.VMEM((SUBLANES, LANES), F32), pltpu.VMEM((SUBLANES, LANES), F32)],
        compiler_params=_cparams(("arbitrary", "arbitrary")),
        name="route",
    )(idx)


DISPATCH_BUFS = 3


def _dispatch_kernel(cnt_ref, pst_ref, dest_ref, h_hbm, xs_ref, hbuf, zbuf, in_sem, out_sem, sem,
                     *, tm, blk, c, nsteps):
    i = pl.program_id(0)
    cur = i % DISPATCH_BUFS
    nxt = (i + 1) % DISPATCH_BUFS

    def load(step, b):
        return pltpu.make_async_copy(h_hbm.at[pl.ds(pl.multiple_of(step * (tm * c), tm * c), tm * c), :],
                                     hbuf.at[b], in_sem.at[b])

    def drain(b):
        def body(rb, carry):
            for _ in range(DMA_UNROLL * TOP_K):
                _slot_copy(hbuf.at[b], 0, xs_ref, 0, out_sem.at[b], c).wait()
            return carry

        lax.fori_loop(0, tm // DMA_UNROLL, body, 0)

    @pl.when(i == 0)
    def _():
        load(0, 0).start()

    @pl.when(i >= DISPATCH_BUFS - 1)
    def _():
        drain(nxt)

    @pl.when(i + 1 < nsteps)
    def _():
        load(i + 1, nxt).start()

    load(i, cur).wait()

    def issue(rb, carry):
        for u in range(DMA_UNROLL):
            r = rb * DMA_UNROLL + u
            for kk in range(TOP_K):
                _slot_copy(hbuf.at[cur], r, xs_ref, dest_ref[r * TOP_K + kk], out_sem.at[cur],
                           c).start(priority=kk % 2)
        return carry

    lax.fori_loop(0, tm // DMA_UNROLL, issue, 0)

    @pl.when(i == nsteps - 1)
    def _():
        for back in range(min(DISPATCH_BUFS - 1, nsteps)):
            drain((i - back) % DISPATCH_BUFS)
        zbuf[...] = jnp.zeros(zbuf.shape, F32)

        def per_expert(e, carry):
            n = cnt_ref[e]
            base = pst_ref[e]
            n_pad = lax.shift_right_logical(n + (blk - 1), int(np.log2(blk))) * blk

            def fill(r, carry2):
                _slot_copy(zbuf, 0, xs_ref, base + r, sem, c).start()
                return carry2

            def fill_wait(r, carry2):
                _slot_copy(zbuf, 0, xs_ref, 0, sem, c).wait()
                return carry2

            lax.fori_loop(n, n_pad, fill, 0)
            lax.fori_loop(n, n_pad, fill_wait, 0)
            return carry

        lax.fori_loop(0, N_EXPERTS, per_expert, 0)


def _dispatch(counts, pstart, dest, h_tiles, n_pad, blk, D):
    c = D // LANES
    T = h_tiles.shape[0] // c
    tm = min(ROW_TILE, T)
    assert tm % DMA_UNROLL == 0
    kern = functools.partial(_dispatch_kernel, tm=tm, blk=blk, c=c, nsteps=T // tm)
    return pl.pallas_call(
        kern,
        grid_spec=pltpu.PrefetchScalarGridSpec(
            num_scalar_prefetch=2,
            grid=(T // tm,),
            in_specs=[pl.BlockSpec((tm * TOP_K,), lambda i, *_: (i,), memory_space=pltpu.SMEM),
                      pl.BlockSpec(memory_space=pl.ANY)],
            out_specs=pl.BlockSpec(memory_space=pl.ANY),
            scratch_shapes=[pltpu.VMEM((DISPATCH_BUFS, tm * c, LANES), F32), pltpu.VMEM((c, LANES), F32),
                            pltpu.SemaphoreType.DMA((DISPATCH_BUFS,)),
                            pltpu.SemaphoreType.DMA((DISPATCH_BUFS,)),
                            pltpu.SemaphoreType.DMA(())]),
        out_shape=jax.ShapeDtypeStruct((n_pad * c, LANES), F32),
        compiler_params=_cparams(("arbitrary",)),
        name="dispatch",
    )(counts, pstart, dest, h_tiles)


def _expert_kernel(be_ref, nv_ref, x_ref, wgu_ref, bgu_ref, wd_ref, bd_ref, y_ref, wgu_bf, wd_bf,
                   *, blk, d, de):
    j = pl.program_id(0)

    @pl.when((j == 0) | (be_ref[j] != be_ref[jnp.maximum(j - 1, 0)]))
    def _():
        wgu_bf[...] = wgu_ref[0].astype(BF16)
        wd_bf[...] = wd_ref[0].astype(BF16)

    @pl.when(j < nv_ref[0])
    def _():
        x = _from_slot_tiles(x_ref, blk, d)
        gu = jnp.dot(x.astype(BF16), wgu_bf[...], preferred_element_type=F32) + bgu_ref[0]
        gate = jnp.minimum(gu[:, :de], SWIGLU_LIMIT)
        up = jnp.clip(gu[:, de:], -SWIGLU_LIMIT, SWIGLU_LIMIT)
        act = (up + 1.0) * (gate * jax.nn.sigmoid(SWIGLU_ALPHA * gate))
        y = jnp.dot(act.astype(BF16), wd_bf[...], preferred_element_type=F32) + bd_ref[0]
        _to_slot_tiles(y_ref, y)


def _expert_ffn(block_e, nvalid, xs, wgu, bgu, wd, bd, blk):
    de, D = wd.shape[1], wd.shape[2]
    c = D // LANES
    n_pad = xs.shape[0] // c
    kern = functools.partial(_expert_kernel, blk=blk, d=D, de=de)
    row = lambda j, be, nv: (jnp.minimum(j, nv[0] - 1), 0)
    exp = lambda j, be, nv: (be[jnp.minimum(j, nv[0] - 1)], 0, 0)
    return pl.pallas_call(
        kern,
        grid_spec=pltpu.PrefetchScalarGridSpec(
            num_scalar_prefetch=2,
            grid=(n_pad // blk,),
            in_specs=[pl.BlockSpec((blk * c, LANES), row),
                      pl.BlockSpec((1, D, 2 * de), exp),
                      pl.BlockSpec((1, 1, 2 * de), exp),
                      pl.BlockSpec((1, de, D), exp),
                      pl.BlockSpec((1, 1, D), exp)],
            out_specs=pl.BlockSpec((blk * c, LANES), row),
            scratch_shapes=[pltpu.VMEM((D, 2 * de), BF16), pltpu.VMEM((de, D), BF16)]),
        out_shape=jax.ShapeDtypeStruct((n_pad * c, LANES), F32),
        compiler_params=_cparams(("arbitrary",)),
        name="expert_ffn",
    )(block_e, nvalid, xs, wgu, bgu, wd, bd)


def _combine_kernel(dest_ref, dnext_ref, x1_ref, gate_ref, g_ref, y_ref, o_ref, ybuf, sem, *, tm, d):
    c = d // LANES
    i = pl.program_id(0)
    cur = i % 2

    def issue(dref, half):
        def body(rb, carry):
            for u in range(DMA_UNROLL):
                r = rb * DMA_UNROLL + u
                for kk in range(TOP_K):
                    _slot_copy(y_ref, dref[r * TOP_K + kk], ybuf.at[half, kk], r, sem.at[half],
                               c).start(priority=kk % 2)
            return carry

        lax.fori_loop(0, tm // DMA_UNROLL, body, 0)

    @pl.when(i == 0)
    def _():
        issue(dest_ref, 0)

    @pl.when(i + 1 < pl.num_programs(0))
    def _():
        issue(dnext_ref, 1 - cur)

    def drain(rb, carry):
        for _ in range(DMA_UNROLL * TOP_K):
            _slot_copy(y_ref, 0, ybuf.at[cur, 0], 0, sem.at[cur], c).wait()
        return carry

    lax.fori_loop(0, tm // DMA_UNROLL, drain, 0)

    gates = gate_ref[...]
    acc = x1_ref[...]
    for kk in range(TOP_K):
        acc = acc + gates[:, kk:kk + 1] * _from_slot_tiles(ybuf.at[cur, kk], tm, d)
    o_ref[...] = _rms(acc, g_ref[...])


def _combine(dest, x1, gates, g, y):
    T, D = x1.shape
    c = D // LANES
    tm = min(ROW_TILE, T)
    kern = functools.partial(_combine_kernel, tm=tm, d=D)
    last = T // tm - 1
    return pl.pallas_call(
        kern,
        grid=(T // tm,),
        in_specs=[pl.BlockSpec((tm * TOP_K,), lambda i: (i,), memory_space=pltpu.SMEM),
                  pl.BlockSpec((tm * TOP_K,), lambda i: (jnp.minimum(i + 1, last),),
                               memory_space=pltpu.SMEM),
                  pl.BlockSpec((tm, D), lambda i: (i, 0)),
                  pl.BlockSpec((tm, LANES), lambda i: (i, 0)),
                  pl.BlockSpec((1, D), lambda i: (0, 0)),
                  pl.BlockSpec(memory_space=pl.ANY)],
        out_specs=pl.BlockSpec((tm, D), lambda i: (i, 0)),
        out_shape=jax.ShapeDtypeStruct((T, D), F32),
        scratch_shapes=[pltpu.VMEM((2, TOP_K, tm * c, LANES), F32), pltpu.SemaphoreType.DMA((2,))],
        compiler_params=_cparams(("arbitrary",)),
        name="combine",
    )(dest, dest, x1, gates, g, y)


def kernel(x, mem, rel_bias, attn_norm_g, w_mix_in, conv_w, lambda_qk, diff_subln_g, mem_norm_g,
           w_mem_kv, w_mix_out, ffn_norm_g, router_w, router_b, w_gate_up, b_gate_up, w_down,
           b_down, final_norm_g):
    B, S, D = x.shape
    T = B * S
    E = router_w.shape[-1]
    assert w_mix_in.shape[0] == 1 and E == N_EXPERTS

    km, vm = _mem_kv(mem, mem_norm_g[0][None, :], w_mem_kv[0].astype(BF16))
    t = min(ATTN_TILE, S)
    yc, ym, qt, k, vt = _mix_in(x, attn_norm_g[0][None, :], w_mix_in[0].astype(BF16), conv_w[0],
                                km, vm, t)
    bias = _bias_tiles(rel_bias, t // 2)
    yd = _diff_attn(qt, k, vt, bias, lambda_qk[0], diff_subln_g[0][None, :], t)

    rw_pad = jnp.zeros((D, LANES), F32).at[:, :E].set(router_w[0])
    rw_hi = rw_pad.astype(BF16)
    rw_lo = (rw_pad - rw_hi.astype(F32)).astype(BF16)
    rw_pad = jnp.concatenate([rw_hi, rw_lo], axis=1)
    rb_pad = jnp.full((1, LANES), NEG, F32).at[0, :E].set(router_b[0])
    x1, h, idx, gates = _mix_out(x.reshape(T, D), yc.reshape(T, -1), yd.reshape(T, -1),
                                 ym.reshape(T, -1), w_mix_out[0].astype(BF16),
                                 ffn_norm_g[0][None, :], rw_pad, rb_pad)

    blk = MOE_BLOCK
    n_blocks = (T * TOP_K) // blk + E
    n_pad = n_blocks * blk
    pos, meta = _route(idx, blk)
    dest = pos[:, :TOP_K].reshape(T * TOP_K)
    counts = meta[0, :E].astype(I32)
    pstart = meta[1, :E].astype(I32)
    pend = meta[2, :E].astype(I32)
    block_start = jnp.arange(n_blocks, dtype=I32) * blk
    block_e = jnp.minimum(jnp.sum(pend[None, :] <= block_start[:, None], axis=1), E - 1).astype(I32)
    nvalid = (pend[E - 1:E] // blk).astype(I32)

    xs = _dispatch(counts, pstart, dest, h, n_pad, blk, D)
    y = _expert_ffn(block_e, nvalid, xs, w_gate_up[0], b_gate_up[0][:, None, :],
                    w_down[0], b_down[0][:, None, :], blk)
    out = _combine(dest, x1, gates, final_norm_g[None, :], y)
    return out.reshape(B, S, D)
```
